```python
import jax
import jax.numpy as jnp
from jax import lax
import numpy as np

D_MODEL = 1024
BATCH = 4
SEQ = 8192
DEPTH = 4

GRID_W = 64
CTX_LEN = 256
Q_BLOCK = 128
ROPE_THETA = 10000.0
HEAD_DIM = 64
A_HEADS = 8
A_KV_HEADS = 2
MLA_HEADS = 8
MLA_Q_RANK = 384
MLA_KV_RANK = 256
MLA_NOPE = 64
MLA_ROPE = 32
MLA_V = 64
NA_HEADS = 8
NBR_ROWS = 8
NBR_COLS = 16
HG_HEADS = 4
HG_EXPAND = 128
HG_WIDTH = HG_HEADS * HG_EXPAND
HG_CHUNK = 64
D_FF = 2816
N_EXPERTS = 8
TOP_K = 2
D_FF_EXPERT = 3584
ALPHA = (2 * DEPTH) ** 0.25
BETA = (8 * DEPTH) ** -0.25
N_EVEN = (DEPTH + 1) // 2
N_ODD = DEPTH // 2
EV_SIZES = (A_HEADS * HEAD_DIM, A_KV_HEADS * HEAD_DIM, A_KV_HEADS * HEAD_DIM, MLA_Q_RANK, MLA_KV_RANK, MLA_ROPE)
EV_IN = sum(EV_SIZES)
EV_OUT = A_HEADS * HEAD_DIM + MLA_HEADS * MLA_V
OD_SIZES = (NA_HEADS * HEAD_DIM,) * 3 + (HG_WIDTH,) * 5
OD_IN = sum(OD_SIZES)
OD_OUT = NA_HEADS * HEAD_DIM + HG_WIDTH
F32 = jnp.float32

kernel_name = 'hybrid_dit_gqa_mla_natten_hgrn2_moe'


def split_cols(p, sizes):
    out, start = [], 0
    for s in sizes:
        out.append(p[..., start:start + s])
        start += s
    return out


def rms_norm(x, g, eps=1e-6):
    xf = x.astype(F32)
    y = xf * lax.rsqrt(jnp.mean(xf * xf, axis=-1, keepdims=True) + eps)
    return (y * g.astype(F32)).astype(x.dtype)


def layer_norm(x, g, b, eps=1e-6):
    xf = x.astype(F32)
    mu = jnp.mean(xf, axis=-1, keepdims=True)
    var = jnp.mean(jnp.square(xf - mu), axis=-1, keepdims=True)
    return ((xf - mu) * lax.rsqrt(var + eps) * g.astype(F32) + b.astype(F32)).astype(x.dtype)


def modulate(x, shift, scale):
    return x * (1 + scale) + shift


def axial_rope_tables(n_tokens, rot_dim, dtype):
    axis_dim = rot_dim // 2
    inv_freq = ROPE_THETA ** (-jnp.arange(0, axis_dim, 2, dtype=F32) / axis_dim)
    t = jnp.arange(n_tokens, dtype=jnp.int32)
    pos = jnp.stack([t // GRID_W, t % GRID_W], axis=-1).astype(F32)
    ang = pos[:, :, None] * inv_freq
    ang = jnp.concatenate([ang, ang], axis=-1).reshape(n_tokens, rot_dim)
    return jnp.cos(ang).astype(dtype), jnp.sin(ang).astype(dtype)


def apply_axial_rope(x, cos, sin):
    rot_dim = x.shape[-1]
    axis_dim = rot_dim // 2
    half = axis_dim // 2
    xs = x.reshape(x.shape[:-1] + (2, axis_dim))
    rot = jnp.concatenate([-xs[..., half:], xs[..., :half]], axis=-1).reshape(x.shape)
    return x * cos[None, :, None, :] + rot * sin[None, :, None, :]


def blocked_attention(q, k, v):
    B, Sq, G, R, Dk = q.shape
    nb = Sq // Q_BLOCK
    scale = Dk ** -0.5
    qb = jnp.moveaxis(q.reshape(B, nb, Q_BLOCK, G, R, Dk), 1, 0)

    def one_block(qblk):
        s = jnp.einsum('bqgrd,bkgd->bgrqk', qblk, k).astype(F32) * scale
        p = jax.nn.softmax(s, axis=-1).astype(v.dtype)
        return jnp.einsum('bgrqk,bkgd->bqgrd', p, v)

    o = lax.map(one_block, qb)
    return jnp.moveaxis(o, 0, 1).reshape(B, Sq, G, R, v.shape[-1])


def neighbourhood_attention(q, k, v, k_ctx, v_ctx, rpb):
    B, S, H, Dh = q.shape
    rows = S // GRID_W
    kh = min(NBR_ROWS, rows)
    kw = NBR_COLS
    r = np.arange(rows)
    row_start = np.clip(r - kh // 2, 0, rows - kh)
    dr_idx = row_start[:, None] + np.arange(kh)[None, :] - r[:, None] + (NBR_ROWS - 1)
    cc = np.arange(GRID_W)
    col_start = np.clip(cc - kw // 2, 0, GRID_W - kw)
    col_idx = col_start[:, None] + np.arange(kw)[None, :]
    dc_idx = col_idx - cc[:, None] + (NBR_COLS - 1)
    rpb_c = rpb[:, :, dc_idx]
    scale = Dh ** -0.5
    q_rows = jnp.moveaxis(q.reshape(B, rows, GRID_W, H, Dh), 1, 0)
    kg = k.reshape(B, rows, GRID_W, H, Dh)
    vg = v.reshape(B, rows, GRID_W, H, Dh)
    n_win = kh * kw

    def one_row(args):
        q_r, r0, dr = args
        kb = lax.dynamic_slice_in_dim(kg, r0, kh, axis=1)
        vb = lax.dynamic_slice_in_dim(vg, r0, kh, axis=1)
        kwin = kb[:, :, col_idx]
        vwin = vb[:, :, col_idx]
        bias = jnp.take(rpb_c, dr, axis=1).transpose(0, 2, 1, 3).astype(F32)
        s_win = jnp.einsum('bchd,bicjhd->bhcij', q_r, kwin).astype(F32) * scale + bias[None]
        s_ctx = jnp.einsum('bchd,bkhd->bhck', q_r, k_ctx).astype(F32) * scale
        s = jnp.concatenate([s_win.reshape(B, H, GRID_W, n_win), s_ctx], axis=-1)
        p = jax.nn.softmax(s, axis=-1).astype(v.dtype)
        p_win = p[..., :n_win].reshape(B, H, GRID_W, kh, kw)
        p_ctx = p[..., n_win:]
        return (jnp.einsum('bhcij,bicjhd->bchd', p_win, vwin)
                + jnp.einsum('bhck,bkhd->bchd', p_ctx, v_ctx))

    o = lax.map(one_row, (q_rows, jnp.asarray(row_start, jnp.int32), jnp.asarray(dr_idx, jnp.int32)))
    return jnp.moveaxis(o, 0, 1).reshape(B, S, H, Dh)


def gla_chunk_scan(q, k, v, log_f, s0, with_outputs):
    B, L, H, E = q.shape
    n = L // HG_CHUNK

    def to_chunks(t):
        return t.reshape(B, n, HG_CHUNK, H, E).transpose(1, 0, 3, 2, 4)

    causal = jnp.tril(jnp.ones((HG_CHUNK, HG_CHUNK), dtype=bool))

    def step(state, blk):
        qc, kc, vc, gc = blk
        b = jnp.cumsum(gc, axis=2)
        b_end = b[:, :, -1, :]
        k_end = kc * jnp.exp(b_end[:, :, None, :] - b)
        new_state = jnp.exp(b_end)[..., None] * state + jnp.einsum('bhse,bhsf->bhef', k_end, vc)
        if not with_outputs:
            return new_state, None
        diff = b[:, :, :, None, :] - b[:, :, None, :, :]
        decay = jnp.exp(jnp.where(causal[:, :, None], diff, -jnp.inf))
        scores = jnp.einsum('bhte,bhtse,bhse->bhts', qc, decay, kc)
        out = (jnp.einsum('bhts,bhsf->bhtf', scores, vc)
               + jnp.einsum('bhte,bhef->bhtf', qc * jnp.exp(b), state))
        return new_state, out

    final, outs = lax.scan(step, s0, (to_chunks(q), to_chunks(k), to_chunks(v), to_chunks(log_f)))
    if not with_outputs:
        return final, None
    return final, outs.transpose(1, 0, 3, 2, 4).reshape(B, L, H, E)


def hgrn_forget(z, lb):
    zf = z.astype(F32)
    lb = lb.reshape(HG_HEADS, HG_EXPAND)
    f = lb + (1.0 - lb) * jax.nn.sigmoid(zf)
    return (1.0 - lb) * jax.nn.sigmoid(-zf), jnp.log(f)


def hgrn2_bidirectional(lat, ctx, need_ctx):
    ql, vl, kfl, gfl, kbl, gbl = lat
    qc, vc, kfc, gfc, kbc, gbc = ctx
    B = ql.shape[0]
    s0 = jnp.zeros((B, HG_HEADS, HG_EXPAND, HG_EXPAND), F32)
    flip = lambda t: jnp.flip(t, axis=1)
    s_cf, o_cf = gla_chunk_scan(qc, kfc, vc, gfc, s0, need_ctx)
    _, o_lf = gla_chunk_scan(ql, kfl, vl, gfl, s_cf, True)
    s_cb, o_cb = gla_chunk_scan(flip(qc), flip(kbc), flip(vc), flip(gbc), s0, need_ctx)
    _, o_lb = gla_chunk_scan(flip(ql), flip(kbl), flip(vl), flip(gbl), s_cb, True)
    o_l = o_lf + flip(o_lb)
    o_c = (o_cf + flip(o_cb)) if need_ctx else None
    return o_l, o_c


def hgrn_readout(o, g, onorm_g):
    B, L = g.shape[:2]
    y = rms_norm(o, onorm_g) * jax.nn.silu(g.astype(F32))
    return y.astype(g.dtype).reshape(B, L, HG_WIDTH)


def attn_mla_mixer(hl, hc, w_in, qn_g, kn_g, cq_g, ckv_g, w_uq, w_ukv, w_out, rope_a, rope_b, need_ctx):
    def project(h, with_rope):
        B, L, _ = h.shape
        qa, ka, va, cq, ckv, kr = split_cols(h @ w_in, EV_SIZES)
        qa = rms_norm(qa.reshape(B, L, A_HEADS, HEAD_DIM), qn_g)
        ka = rms_norm(ka.reshape(B, L, A_KV_HEADS, HEAD_DIM), kn_g)
        va = va.reshape(B, L, A_KV_HEADS, HEAD_DIM)
        qb = (rms_norm(cq, cq_g) @ w_uq).reshape(B, L, MLA_HEADS, MLA_NOPE + MLA_ROPE)
        kvb = (rms_norm(ckv, ckv_g) @ w_ukv).reshape(B, L, MLA_HEADS, MLA_NOPE + MLA_V)
        q_rope = qb[..., MLA_NOPE:]
        k_rope = kr.reshape(B, L, 1, MLA_ROPE)
        if with_rope:
            qa = apply_axial_rope(qa, *rope_a)
            ka = apply_axial_rope(ka, *rope_a)
            q_rope = apply_axial_rope(q_rope, *rope_b)
            k_rope = apply_axial_rope(k_rope, *rope_b)
        qb = jnp.concatenate([qb[..., :MLA_NOPE], q_rope], axis=-1)
        kb = jnp.concatenate([kvb[..., :MLA_NOPE], jnp.broadcast_to(k_rope, (B, L, MLA_HEADS, MLA_ROPE))], axis=-1)
        vb = kvb[..., MLA_NOPE:]
        qa = qa.reshape(B, L, A_KV_HEADS, A_HEADS // A_KV_HEADS, HEAD_DIM)
        return qa, ka, va, qb[:, :, :, None, :], kb, vb

    B, S, _ = hl.shape
    qa_l, ka_l, va_l, qb_l, kb_l, vb_l = project(hl, True)
    qa_c, ka_c, va_c, qb_c, kb_c, vb_c = project(hc, False)
    oa_l = blocked_attention(qa_l, jnp.concatenate([ka_l, ka_c], axis=1), jnp.concatenate([va_l, va_c], axis=1))
    ob_l = blocked_attention(qb_l, jnp.concatenate([kb_l, kb_c], axis=1), jnp.concatenate([vb_l, vb_c], axis=1))
    yl = jnp.concatenate([oa_l.reshape(B, S, A_HEADS * HEAD_DIM), ob_l.reshape(B, S, MLA_HEADS * MLA_V)], axis=-1) @ w_out
    if not need_ctx:
        return yl, None
    Lc = hc.shape[1]
    oa_c = blocked_attention(qa_c, ka_c, va_c)
    ob_c = blocked_attention(qb_c, kb_c, vb_c)
    yc = jnp.concatenate([oa_c.reshape(B, Lc, A_HEADS * HEAD_DIM), ob_c.reshape(B, Lc, MLA_HEADS * MLA_V)], axis=-1) @ w_out
    return yl, yc


def na_hgrn_mixer(hl, hc, w_in, rpb, lb_fwd, lb_bwd, onorm_g, w_out, need_ctx):
    def project(h):
        B, L, _ = h.shape
        parts = split_cols(h @ w_in, OD_SIZES)
        qn, kn, vn = [p.reshape(B, L, NA_HEADS, HEAD_DIM) for p in parts[:3]]
        qd, vd, zf, zb, gd = [p.reshape(B, L, HG_HEADS, HG_EXPAND) for p in parts[3:]]
        k_f, logf_f = hgrn_forget(zf, lb_fwd)
        k_b, logf_b = hgrn_forget(zb, lb_bwd)
        hg = (jax.nn.silu(qd).astype(F32), vd.astype(F32), k_f, logf_f, k_b, logf_b)
        return (qn, kn, vn), hg, gd

    B, S, _ = hl.shape
    (qn_l, kn_l, vn_l), hg_l, g_l = project(hl)
    (qn_c, kn_c, vn_c), hg_c, g_c = project(hc)
    on_l = neighbourhood_attention(qn_l, kn_l, vn_l, kn_c, vn_c, rpb)
    od_l, od_c = hgrn2_bidirectional(hg_l, hg_c, need_ctx)
    yl = jnp.concatenate([on_l.reshape(B, S, NA_HEADS * HEAD_DIM), hgrn_readout(od_l, g_l, onorm_g)], axis=-1) @ w_out
    if not need_ctx:
        return yl, None
    Lc = hc.shape[1]
    on_c = blocked_attention(qn_c[:, :, :, None, :], kn_c, vn_c)
    yc = jnp.concatenate([on_c.reshape(B, Lc, NA_HEADS * HEAD_DIM), hgrn_readout(od_c, g_c, onorm_g)], axis=-1) @ w_out
    return yl, yc


def swiglu(h, wg, wu, wd):
    return (jax.nn.silu(h @ wg) * (h @ wu)) @ wd


def moe_ffn(h, router, wg, wu, wd):
    logits = (h @ router).astype(F32)
    top_v, top_i = lax.top_k(logits, TOP_K)
    w = jax.nn.softmax(top_v, axis=-1)
    gates = jnp.sum(jax.nn.one_hot(top_i, N_EXPERTS, dtype=F32) * w[..., None], axis=-2).astype(h.dtype)
    y = jnp.zeros_like(h)
    for e in range(N_EXPERTS):
        y = y + gates[..., e:e + 1] * swiglu(h, wg[e], wu[e], wd[e])
    return y


def setup_inputs(seed: int = 0) -> dict:
    key = jax.random.key(seed)
    ks = iter(jax.random.split(key, 40))
    D = D_MODEL

    def nrm(shape, scale):
        return jax.random.normal(next(ks), shape, F32) * scale

    def gain(shape):
        return 1.0 + nrm(shape, 0.02)

    return {
        'x': nrm((BATCH, SEQ, D), 1.0),
        'c': nrm((BATCH, D), 1.0),
        'ctx': nrm((BATCH, CTX_LEN, D), 1.0),
        'c_ctx': nrm((D,), 1.0),
        'ada_w': nrm((DEPTH, D, 6 * D), 0.5 * D ** -0.5),
        'ada_b': nrm((DEPTH, 6 * D), 0.02),
        'ln1_g': gain((DEPTH, D)),
        'ln1_b': nrm((DEPTH, D), 0.02),
        'ln2_g': gain((DEPTH, D)),
        'ln2_b': nrm((DEPTH, D), 0.02),
        'ev_w_in': nrm((N_EVEN, D, EV_IN), D ** -0.5),
        'ev_qn_g': gain((N_EVEN, HEAD_DIM)),
        'ev_kn_g': gain((N_EVEN, HEAD_DIM)),
        'ev_cq_g': gain((N_EVEN, MLA_Q_RANK)),
        'ev_ckv_g': gain((N_EVEN, MLA_KV_RANK)),
        'ev_w_uq': nrm((N_EVEN, MLA_Q_RANK, MLA_HEADS * (MLA_NOPE + MLA_ROPE)), MLA_Q_RANK ** -0.5),
        'ev_w_ukv': nrm((N_EVEN, MLA_KV_RANK, MLA_HEADS * (MLA_NOPE + MLA_V)), MLA_KV_RANK ** -0.5),
        'ev_w_out': nrm((N_EVEN, EV_OUT, D), BETA * EV_OUT ** -0.5),
        'ff_w_gate': nrm((N_EVEN, D, D_FF), D ** -0.5),
        'ff_w_up': nrm((N_EVEN, D, D_FF), D ** -0.5),
        'ff_w_down': nrm((N_EVEN, D_FF, D), BETA * D_FF ** -0.5),
        'od_w_in': nrm((N_ODD, D, OD_IN), D ** -0.5),
        'od_rpb': nrm((N_ODD, NA_HEADS, 2 * NBR_ROWS - 1, 2 * NBR_COLS - 1), 0.02),
        'od_lb_logits': nrm((2, N_ODD, HG_WIDTH), 0.1),
        'od_onorm_g': gain((N_ODD, HG_EXPAND)),
        'od_w_out': nrm((N_ODD, OD_OUT, D), BETA * OD_OUT ** -0.5),
        'moe_router': nrm((N_ODD, D, N_EXPERTS), D ** -0.5),
        'moe_w_gate': nrm((N_ODD, N_EXPERTS, D, D_FF_EXPERT), D ** -0.5),
        'moe_w_up': nrm((N_ODD, N_EXPERTS, D, D_FF_EXPERT), D ** -0.5),
        'moe_w_down': nrm((N_ODD, N_EXPERTS, D_FF_EXPERT, D), BETA * D_FF_EXPERT ** -0.5),
    }


def reference(x, c, ctx, c_ctx, ada_w, ada_b, ln1_g, ln1_b, ln2_g, ln2_b,
              ev_w_in, ev_qn_g, ev_kn_g, ev_cq_g, ev_ckv_g, ev_w_uq, ev_w_ukv, ev_w_out,
              ff_w_gate, ff_w_up, ff_w_down,
              od_w_in, od_rpb, od_lb_logits, od_onorm_g, od_w_out,
              moe_router, moe_w_gate, moe_w_up, moe_w_down):
    B, S, _ = x.shape
    rope_a = axial_rope_tables(S, HEAD_DIM, x.dtype)
    rope_b = axial_rope_tables(S, MLA_ROPE, x.dtype)
    lb_p = jax.nn.softmax(od_lb_logits.astype(F32), axis=1)
    lb_tab = jnp.cumsum(lb_p, axis=1) - lb_p[:, :1]
    s_lat = jax.nn.silu(c)
    s_ctx = jax.nn.silu(c_ctx)
    xl, xc = x, ctx

    for layer in range(DEPTH):
        need_ctx = layer < DEPTH - 1
        j = layer // 2
        mod_l = jnp.split((s_lat @ ada_w[layer] + ada_b[layer])[:, None, :], 6, axis=-1)
        mod_c = jnp.split((s_ctx @ ada_w[layer] + ada_b[layer])[None, None, :], 6, axis=-1)
        hl = modulate(xl, mod_l[0], mod_l[1])
        hc = modulate(xc, mod_c[0], mod_c[1])
        if layer % 2 == 0:
            yl, yc = attn_mla_mixer(hl, hc, ev_w_in[j], ev_qn_g[j], ev_kn_g[j], ev_cq_g[j], ev_ckv_g[j],
                                    ev_w_uq[j], ev_w_ukv[j], ev_w_out[j], rope_a, rope_b, need_ctx)
        else:
            yl, yc = na_hgrn_mixer(hl, hc, od_w_in[j], od_rpb[j], lb_tab[0, j], lb_tab[1, j],
                                   od_onorm_g[j], od_w_out[j], need_ctx)

        def channel_mixer(h):
            if layer % 2 == 0:
                return swiglu(h, ff_w_gate[j], ff_w_up[j], ff_w_down[j])
            return moe_ffn(h, moe_router[j], moe_w_gate[j], moe_w_up[j], moe_w_down[j])

        xl = layer_norm(ALPHA * xl + mod_l[2] * yl, ln1_g[layer], ln1_b[layer])
        xl = layer_norm(ALPHA * xl + mod_l[5] * channel_mixer(modulate(xl, mod_l[3], mod_l[4])),
                        ln2_g[layer], ln2_b[layer])
        if need_ctx:
            xc = layer_norm(ALPHA * xc + mod_c[2] * yc, ln1_g[layer], ln1_b[layer])
            xc = layer_norm(ALPHA * xc + mod_c[5] * channel_mixer(modulate(xc, mod_c[3], mod_c[4])),
                            ln2_g[layer], ln2_b[layer])
    return xl
```

```python
import functools

import numpy as np
import jax
import jax.numpy as jnp
from jax import lax
from jax.experimental import pallas as pl
from jax.experimental.pallas import tpu as pltpu

F32 = jnp.float32
BF16 = jnp.bfloat16

D_MODEL = 1024
DEPTH = 4
GRID_W = 64
Q_BLOCK = 128
ROPE_THETA = 10000.0
HEAD_DIM = 64
A_HEADS = 8
A_KV_HEADS = 2
MLA_HEADS = 8
MLA_Q_RANK = 384
MLA_KV_RANK = 256
MLA_NOPE = 64
MLA_ROPE = 32
MLA_V = 64
NA_HEADS = 8
NBR_ROWS = 8
NBR_COLS = 16
HG_HEADS = 4
HG_EXPAND = 128
HG_WIDTH = HG_HEADS * HG_EXPAND
HG_CHUNK = 64
N_EXPERTS = 8
TOP_K = 2
ALPHA = (2 * DEPTH) ** 0.25
EV_SIZES = (A_HEADS * HEAD_DIM, A_KV_HEADS * HEAD_DIM, A_KV_HEADS * HEAD_DIM, MLA_Q_RANK, MLA_KV_RANK, MLA_ROPE)
OD_SIZES = (NA_HEADS * HEAD_DIM,) * 3 + (HG_WIDTH,) * 5

V7X_VMEM_LIMIT_BYTES = 56 * 1024 * 1024
ROW_TILE = 512
MOE_ROW_TILE = 512


def _cparams(*sem):
    return pltpu.CompilerParams(dimension_semantics=sem, vmem_limit_bytes=V7X_VMEM_LIMIT_BYTES)


def _mm_body(x_ref, w_ref, o_ref):
    o_ref[...] = jnp.dot(x_ref[...].astype(BF16), w_ref[...], preferred_element_type=F32).astype(o_ref.dtype)


def _mm(x, w, *, tm=ROW_TILE, out_dtype=F32, name="mm"):
    m, k = x.shape
    n = w.shape[1]
    assert m % tm == 0, (m, tm)
    return pl.pallas_call(
        _mm_body,
        grid=(m // tm,),
        in_specs=[pl.BlockSpec((tm, k), lambda i: (i, 0)), pl.BlockSpec((k, n), lambda i: (0, 0))],
        out_specs=pl.BlockSpec((tm, n), lambda i: (i, 0)),
        out_shape=jax.ShapeDtypeStruct((m, n), out_dtype),
        compiler_params=_cparams("parallel"),
        name=name,
    )(x, w)


def _router_body(x_ref, w_ref, o_ref):
    o_ref[...] = jnp.dot(x_ref[...], w_ref[...], preferred_element_type=F32, precision=lax.Precision.HIGHEST)


def _router_logits(h, router):
    m, k = h.shape
    e = router.shape[1]
    return pl.pallas_call(
        _router_body,
        grid=(m // ROW_TILE,),
        in_specs=[pl.BlockSpec((ROW_TILE, k), lambda i: (i, 0)), pl.BlockSpec((k, e), lambda i: (0, 0))],
        out_specs=pl.BlockSpec((ROW_TILE, e), lambda i: (i, 0)),
        out_shape=jax.ShapeDtypeStruct((m, e), F32),
        compiler_params=_cparams("parallel"),
        name="router",
    )(h, router)


def _ffn_body(te_ref, nt_ref, x_ref, wg_ref, wu_ref, wd_ref, o_ref, acc_ref, *, n_chunks):
    t = pl.program_id(0)
    c = pl.program_id(1)
    valid = t < nt_ref[0]

    @pl.when(valid)
    def _():
        x = x_ref[...]
        g = jnp.dot(x, wg_ref[0], preferred_element_type=F32)
        u = jnp.dot(x, wu_ref[0], preferred_element_type=F32)
        mid = (g * jax.nn.sigmoid(g) * u).astype(BF16)
        part = jnp.dot(mid, wd_ref[0], preferred_element_type=F32)

        @pl.when(c == 0)
        def _():
            acc_ref[...] = part

        @pl.when(c > 0)
        def _():
            acc_ref[...] += part

    @pl.when(c == n_chunks - 1)
    def _():
        o_ref[...] = jnp.where(valid, acc_ref[...], 0.0).astype(o_ref.dtype)


def _ffn(x, wg, wu, wd, tile_expert, n_tiles, *, tm, tf, out_dtype=F32, name="ffn"):
    p, d = x.shape
    f = wg.shape[2]
    assert p % tm == 0 and f % tf == 0
    nc = f // tf

    def chunk(t, c, te, nt):
        return jnp.where(t < nt[0], c, nc - 1)

    grid_spec = pltpu.PrefetchScalarGridSpec(
        num_scalar_prefetch=2,
        grid=(p // tm, nc),
        in_specs=[
            pl.BlockSpec((tm, d), lambda t, c, te, nt: (t, 0)),
            pl.BlockSpec((1, d, tf), lambda t, c, te, nt: (te[t], 0, chunk(t, c, te, nt))),
            pl.BlockSpec((1, d, tf), lambda t, c, te, nt: (te[t], 0, chunk(t, c, te, nt))),
            pl.BlockSpec((1, tf, d), lambda t, c, te, nt: (te[t], chunk(t, c, te, nt), 0)),
        ],
        out_specs=pl.BlockSpec((tm, d), lambda t, c, te, nt: (t, 0)),
        scratch_shapes=[pltpu.VMEM((tm, d), F32)],
    )
    return pl.pallas_call(
        functools.partial(_ffn_body, n_chunks=nc),
        grid_spec=grid_spec,
        out_shape=jax.ShapeDtypeStruct((p, d), out_dtype),
        compiler_params=_cparams("parallel", "arbitrary"),
        name=name,
    )(tile_expert, n_tiles, x, wg, wu, wd)


def split_cols(p, sizes):
    out, start = [], 0
    for s in sizes:
        out.append(p[..., start:start + s])
        start += s
    return out


def rms_norm(x, g, eps=1e-6):
    xf = x.astype(F32)
    y = xf * lax.rsqrt(jnp.mean(xf * xf, axis=-1, keepdims=True) + eps)
    return (y * g.astype(F32)).astype(x.dtype)


def layer_norm(x, g, b, eps=1e-6):
    xf = x.astype(F32)
    mu = jnp.mean(xf, axis=-1, keepdims=True)
    var = jnp.mean(jnp.square(xf - mu), axis=-1, keepdims=True)
    return ((xf - mu) * lax.rsqrt(var + eps) * g.astype(F32) + b.astype(F32)).astype(x.dtype)


def axial_rope_tables(n_tokens, rot_dim, dtype):
    axis_dim = rot_dim // 2
    inv_freq = ROPE_THETA ** (-jnp.arange(0, axis_dim, 2, dtype=F32) / axis_dim)
    t = jnp.arange(n_tokens, dtype=jnp.int32)
    pos = jnp.stack([t // GRID_W, t % GRID_W], axis=-1).astype(F32)
    ang = pos[:, :, None] * inv_freq
    ang = jnp.concatenate([ang, ang], axis=-1).reshape(n_tokens, rot_dim)
    return jnp.cos(ang).astype(dtype), jnp.sin(ang).astype(dtype)


def apply_axial_rope(x, cos, sin):
    rot_dim = x.shape[-1]
    axis_dim = rot_dim // 2
    half = axis_dim // 2
    xs = x.reshape(x.shape[:-1] + (2, axis_dim))
    rot = jnp.concatenate([-xs[..., half:], xs[..., :half]], axis=-1).reshape(x.shape)
    return x * cos[None, :, None, :] + rot * sin[None, :, None, :]


def blocked_attention(q, k, v):
    B, Sq, G, R, Dk = q.shape
    nb = Sq // Q_BLOCK
    scale = Dk ** -0.5
    qb = jnp.moveaxis(q.reshape(B, nb, Q_BLOCK, G, R, Dk), 1, 0)

    def one_block(qblk):
        s = jnp.einsum('bqgrd,bkgd->bgrqk', qblk, k).astype(F32) * scale
        p = jax.nn.softmax(s, axis=-1).astype(v.dtype)
        return jnp.einsum('bgrqk,bkgd->bqgrd', p, v)

    o = lax.map(one_block, qb)
    return jnp.moveaxis(o, 0, 1).reshape(B, Sq, G, R, v.shape[-1])


def neighbourhood_attention(q, k, v, k_ctx, v_ctx, rpb):
    B, S, H, Dh = q.shape
    rows = S // GRID_W
    kh = min(NBR_ROWS, rows)
    kw = NBR_COLS
    r = np.arange(rows)
    row_start = np.clip(r - kh // 2, 0, rows - kh)
    dr_idx = row_start[:, None] + np.arange(kh)[None, :] - r[:, None] + (NBR_ROWS - 1)
    cc = np.arange(GRID_W)
    col_start = np.clip(cc - kw // 2, 0, GRID_W - kw)
    col_idx = col_start[:, None] + np.arange(kw)[None, :]
    dc_idx = col_idx - cc[:, None] + (NBR_COLS - 1)
    rpb_c = rpb[:, :, dc_idx]
    scale = Dh ** -0.5
    q_rows = jnp.moveaxis(q.reshape(B, rows, GRID_W, H, Dh), 1, 0)
    kg = k.reshape(B, rows, GRID_W, H, Dh)
    vg = v.reshape(B, rows, GRID_W, H, Dh)
    n_win = kh * kw

    def one_row(args):
        q_r, r0, dr = args
        kb = lax.dynamic_slice_in_dim(kg, r0, kh, axis=1)
        vb = lax.dynamic_slice_in_dim(vg, r0, kh, axis=1)
        kwin = kb[:, :, col_idx]
        vwin = vb[:, :, col_idx]
        bias = jnp.take(rpb_c, dr, axis=1).transpose(0, 2, 1, 3).astype(F32)
        s_win = jnp.einsum('bchd,bicjhd->bhcij', q_r, kwin).astype(F32) * scale + bias[None]
        s_ctx = jnp.einsum('bchd,bkhd->bhck', q_r, k_ctx).astype(F32) * scale
        s = jnp.concatenate([s_win.reshape(B, H, GRID_W, n_win), s_ctx], axis=-1)
        p = jax.nn.softmax(s, axis=-1).astype(v.dtype)
        p_win = p[..., :n_win].reshape(B, H, GRID_W, kh, kw)
        p_ctx = p[..., n_win:]
        return (jnp.einsum('bhcij,bicjhd->bchd', p_win, vwin)
                + jnp.einsum('bhck,bkhd->bchd', p_ctx, v_ctx))

    o = lax.map(one_row, (q_rows, jnp.asarray(row_start, jnp.int32), jnp.asarray(dr_idx, jnp.int32)))
    return jnp.moveaxis(o, 0, 1).reshape(B, S, H, Dh)


def gla_chunk_scan(q, k, v, log_f, s0, with_outputs):
    B, L, H, E = q.shape
    n = L // HG_CHUNK

    def to_chunks(t):
        return t.reshape(B, n, HG_CHUNK, H, E).transpose(1, 0, 3, 2, 4)

    causal = jnp.tril(jnp.ones((HG_CHUNK, HG_CHUNK), dtype=bool))

    def step(state, blk):
        qc, kc, vc, gc = blk
        b = jnp.cumsum(gc, axis=2)
        b_end = b[:, :, -1, :]
        k_end = kc * jnp.exp(b_end[:, :, None, :] - b)
        new_state = jnp.exp(b_end)[..., None] * state + jnp.einsum('bhse,bhsf->bhef', k_end, vc)
        if not with_outputs:
            return new_state, None
        diff = b[:, :, :, None, :] - b[:, :, None, :, :]
        decay = jnp.exp(jnp.where(causal[:, :, None], diff, -jnp.inf))
        scores = jnp.einsum('bhte,bhtse,bhse->bhts', qc, decay, kc)
        out = (jnp.einsum('bhts,bhsf->bhtf', scores, vc)
               + jnp.einsum('bhte,bhef->bhtf', qc * jnp.exp(b), state))
        return new_state, out

    final, outs = lax.scan(step, s0, (to_chunks(q), to_chunks(k), to_chunks(v), to_chunks(log_f)))
    if not with_outputs:
        return final, None
    return final, outs.transpose(1, 0, 3, 2, 4).reshape(B, L, H, E)


def hgrn_forget(z, lb):
    zf = z.astype(F32)
    lb = lb.reshape(HG_HEADS, HG_EXPAND)
    f = lb + (1.0 - lb) * jax.nn.sigmoid(zf)
    return (1.0 - lb) * jax.nn.sigmoid(-zf), jnp.log(f)


def hgrn2_bidirectional(lat, ctx):
    ql, vl, kfl, gfl, kbl, gbl = lat
    qc, vc, kfc, gfc, kbc, gbc = ctx
    B = ql.shape[0]
    s0 = jnp.zeros((B, HG_HEADS, HG_EXPAND, HG_EXPAND), F32)
    flip = lambda t: jnp.flip(t, axis=1)
    s_cf, o_cf = gla_chunk_scan(qc, kfc, vc, gfc, s0, True)
    _, o_lf = gla_chunk_scan(ql, kfl, vl, gfl, s_cf, True)
    s_cb, o_cb = gla_chunk_scan(flip(qc), flip(kbc), flip(vc), flip(gbc), s0, True)
    _, o_lb = gla_chunk_scan(flip(ql), flip(kbl), flip(vl), flip(gbl), s_cb, True)
    return o_lf + flip(o_lb), o_cf + flip(o_cb)


def hgrn_readout(o, g, onorm_g):
    B, L = g.shape[:2]
    y = rms_norm(o, onorm_g) * jax.nn.silu(g.astype(F32))
    return y.astype(g.dtype).reshape(B, L, HG_WIDTH)


def _moe(h, router, wg, wu, wd):
    t_rows = h.shape[0]
    tm = MOE_ROW_TILE
    logits = _router_logits(h, router)
    top_v, top_i = lax.top_k(logits, TOP_K)
    w = jax.nn.softmax(top_v, axis=-1)
    e_flat = top_i.reshape(-1).astype(jnp.int32)
    counts = jnp.sum(jax.nn.one_hot(e_flat, N_EXPERTS, dtype=jnp.int32), axis=0)
    padded = ((counts + tm - 1) // tm) * tm
    group_end = jnp.cumsum(padded)
    group_start = group_end - padded
    order = jnp.argsort(e_flat, stable=True).astype(jnp.int32)
    e_sorted = e_flat[order]
    first_sorted = jnp.cumsum(counts) - counts
    dest_sorted = group_start[e_sorted] + (jnp.arange(2 * t_rows, dtype=jnp.int32) - first_sorted[e_sorted])
    p_rows = ((2 * t_rows + N_EXPERTS * (tm - 1)) // tm + 1) * tm
    src = jnp.zeros((p_rows,), jnp.int32).at[dest_sorted].set(order // TOP_K)
    slot = jnp.zeros((2 * t_rows,), jnp.int32).at[order].set(dest_sorted).reshape(t_rows, TOP_K)
    n_tiles = (group_end[-1] // tm).astype(jnp.int32).reshape(1)
    tile_start = jnp.arange(p_rows // tm, dtype=jnp.int32) * tm
    tile_expert = jnp.minimum(jnp.searchsorted(group_end, tile_start, side='right'), N_EXPERTS - 1).astype(jnp.int32)
    xs = jnp.take(h.astype(BF16), src, axis=0)
    ys = _ffn(xs, wg, wu, wd, tile_expert, n_tiles, tm=tm, tf=512, out_dtype=F32, name="moe_ffn")
    y = (w[:, 0:1] * jnp.take(ys, slot[:, 0], axis=0) + w[:, 1:2] * jnp.take(ys, slot[:, 1], axis=0))
    return y


def _dense_ffn(h, wg, wu, wd):
    t_rows = h.shape[0]
    nt = t_rows // ROW_TILE
    return _ffn(h.astype(BF16), wg[None], wu[None], wd[None], jnp.zeros((nt,), jnp.int32),
                jnp.full((1,), nt, jnp.int32), tm=ROW_TILE, tf=1408, name="dense_ffn")


def kernel(x, c, ctx, c_ctx, ada_w, ada_b, ln1_g, ln1_b, ln2_g, ln2_b, ev_w_in, ev_qn_g, ev_kn_g, ev_cq_g, ev_ckv_g, ev_w_uq, ev_w_ukv, ev_w_out, ff_w_gate, ff_w_up, ff_w_down, od_w_in, od_rpb, od_lb_logits, od_onorm_g, od_w_out, moe_router, moe_w_gate, moe_w_up, moe_w_down):
    B, S, D = x.shape
    Lc = ctx.shape[1]
    n_lat = B * S
    rope_a = axial_rope_tables(S, HEAD_DIM, F32)
    rope_b = axial_rope_tables(S, MLA_ROPE, F32)
    lb_p = jax.nn.softmax(od_lb_logits.astype(F32), axis=1)
    lb_tab = jnp.cumsum(lb_p, axis=1) - lb_p[:, :1]
    s_all = jnp.concatenate([jax.nn.silu(c), jax.nn.silu(c_ctx)[None]], axis=0)
    s_pad = jnp.concatenate([s_all, jnp.zeros((8 - (B + 1), D), F32)], axis=0)
    X = jnp.concatenate([x.reshape(n_lat, D), ctx.reshape(B * Lc, D)], axis=0)

    def per_row(m):
        return jnp.concatenate([jnp.repeat(m[:B], S, axis=0), jnp.broadcast_to(m[B:B + 1], (B * Lc, D))], axis=0)

    for layer in range(DEPTH):
        j = layer // 2
        mod = _mm(s_pad, ada_w[layer].astype(BF16), tm=8, name="ada")[:B + 1] + ada_b[layer]
        m0, m1, m2, m3, m4, m5 = [per_row(t) for t in jnp.split(mod, 6, axis=-1)]
        h = X * (1 + m1) + m0
        if layer % 2 == 0:
            proj = _mm(h, ev_w_in[j].astype(BF16), name="ev_in")

            def project(pr, L, with_rope):
                qa, ka, va, cq, ckv, kr = split_cols(pr.reshape(B, L, -1), EV_SIZES)
                qa = rms_norm(qa.reshape(B, L, A_HEADS, HEAD_DIM), ev_qn_g[j])
                ka = rms_norm(ka.reshape(B, L, A_KV_HEADS, HEAD_DIM), ev_kn_g[j])
                va = va.reshape(B, L, A_KV_HEADS, HEAD_DIM)
                cqn = rms_norm(cq, ev_cq_g[j]).reshape(B * L, -1)
                ckvn = rms_norm(ckv, ev_ckv_g[j]).reshape(B * L, -1)
                qb = _mm(cqn, ev_w_uq[j].astype(BF16), tm=min(ROW_TILE, B * L), name="uq").reshape(B, L, MLA_HEADS, MLA_NOPE + MLA_ROPE)
                kvb = _mm(ckvn, ev_w_ukv[j].astype(BF16), tm=min(ROW_TILE, B * L), name="ukv").reshape(B, L, MLA_HEADS, MLA_NOPE + MLA_V)
                q_rope = qb[..., MLA_NOPE:]
                k_rope = kr.reshape(B, L, 1, MLA_ROPE)
                if with_rope:
                    qa = apply_axial_rope(qa, *rope_a)
                    ka = apply_axial_rope(ka, *rope_a)
                    q_rope = apply_axial_rope(q_rope, *rope_b)
                    k_rope = apply_axial_rope(k_rope, *rope_b)
                qb = jnp.concatenate([qb[..., :MLA_NOPE], q_rope], axis=-1)
                kb = jnp.concatenate([kvb[..., :MLA_NOPE], jnp.broadcast_to(k_rope, (B, L, MLA_HEADS, MLA_ROPE))], axis=-1)
                vb = kvb[..., MLA_NOPE:]
                qa = qa.reshape(B, L, A_KV_HEADS, A_HEADS // A_KV_HEADS, HEAD_DIM)
                return qa, ka, va, qb[:, :, :, None, :], kb, vb

            qa_l, ka_l, va_l, qb_l, kb_l, vb_l = project(proj[:n_lat], S, True)
            qa_c, ka_c, va_c, qb_c, kb_c, vb_c = project(proj[n_lat:], Lc, False)
            oa_l = blocked_attention(qa_l, jnp.concatenate([ka_l, ka_c], axis=1), jnp.concatenate([va_l, va_c], axis=1))
            ob_l = blocked_attention(qb_l, jnp.concatenate([kb_l, kb_c], axis=1), jnp.concatenate([vb_l, vb_c], axis=1))
            oa_c = blocked_attention(qa_c, ka_c, va_c)
            ob_c = blocked_attention(qb_c, kb_c, vb_c)
            o_l = jnp.concatenate([oa_l.reshape(n_lat, -1), ob_l.reshape(n_lat, -1)], axis=-1)
            o_c = jnp.concatenate([oa_c.reshape(B * Lc, -1), ob_c.reshape(B * Lc, -1)], axis=-1)
            y = _mm(jnp.concatenate([o_l, o_c], axis=0), ev_w_out[j].astype(BF16), name="ev_out")
        else:
            proj = _mm(h, od_w_in[j].astype(BF16), name="od_in")

            def project(pr, L):
                parts = split_cols(pr.reshape(B, L, -1), OD_SIZES)
                qn, kn, vn = [p.reshape(B, L, NA_HEADS, HEAD_DIM) for p in parts[:3]]
                qd, vd, zf, zb, gd = [p.reshape(B, L, HG_HEADS, HG_EXPAND) for p in parts[3:]]
                k_f, logf_f = hgrn_forget(zf, lb_tab[0, j])
                k_b, logf_b = hgrn_forget(zb, lb_tab[1, j])
                hg = (jax.nn.silu(qd).astype(F32), vd.astype(F32), k_f, logf_f, k_b, logf_b)
                return (qn, kn, vn), hg, gd

            (qn_l, kn_l, vn_l), hg_l, g_l = project(proj[:n_lat], S)
            (qn_c, kn_c, vn_c), hg_c, g_c = project(proj[n_lat:], Lc)
            on_l = neighbourhood_attention(qn_l, kn_l, vn_l, kn_c, vn_c, od_rpb[j])
            od_l, od_c = hgrn2_bidirectional(hg_l, hg_c)
            on_c = blocked_attention(qn_c[:, :, :, None, :], kn_c, vn_c)
            o_l = jnp.concatenate([on_l.reshape(n_lat, -1), hgrn_readout(od_l, g_l, od_onorm_g[j]).reshape(n_lat, -1)], axis=-1)
            o_c = jnp.concatenate([on_c.reshape(B * Lc, -1), hgrn_readout(od_c, g_c, od_onorm_g[j]).reshape(B * Lc, -1)], axis=-1)
            y = _mm(jnp.concatenate([o_l, o_c], axis=0), od_w_out[j].astype(BF16), name="od_out")

        X = layer_norm(ALPHA * X + m2 * y, ln1_g[layer], ln1_b[layer])
        h2 = X * (1 + m4) + m3
        if layer % 2 == 0:
            y2 = _dense_ffn(h2, ff_w_gate[j].astype(BF16), ff_w_up[j].astype(BF16), ff_w_down[j].astype(BF16))
        else:
            y2 = _moe(h2, moe_router[j], moe_w_gate[j].astype(BF16), moe_w_up[j].astype(BF16), moe_w_down[j].astype(BF16))
        X = layer_norm(ALPHA * X + m5 * y2, ln2_g[layer], ln2_b[layer])
    return X[:n_lat].reshape(B, S, D)
```
